```python
import jax, jax.numpy as jnp
from jax import lax
import numpy as np

D_MODEL = 1024
BATCH = 4
SEQ = 8192
DEPTH = 1

N_META = 16
D_MIX = D_MODEL
D_SC = D_MIX // 2
D_CF = D_MIX - D_SC
N_GROUPS_SC = 8
N_GROUPS_CF = 8
SC_WIDTH = 3
CF_WIDTH = 31
D_IN = 3 * D_SC + 2 * D_CF
PEER_HEADS = 8
PEER_N_KEYS = 128
PEER_N_EXPERTS = PEER_N_KEYS * PEER_N_KEYS
PEER_D_KEY = 256
PEER_D_HALF = PEER_D_KEY // 2
PEER_TOPK = 16
PEER_CHUNK = 256
EPS = 1e-6

kernel_name = "hymba_conv_peer_block"


def rms_norm(x, g):
    xf = x.astype(jnp.float32)
    y = xf * lax.rsqrt(jnp.mean(xf * xf, axis=-1, keepdims=True) + EPS)
    return (y * g.astype(jnp.float32)).astype(x.dtype)


def layer_norm(x, g, b):
    xf = x.astype(jnp.float32)
    mu = jnp.mean(xf, axis=-1, keepdims=True)
    var = jnp.mean(jnp.square(xf - mu), axis=-1, keepdims=True)
    y = (xf - mu) * lax.rsqrt(var + EPS)
    return (y * g.astype(jnp.float32) + b.astype(jnp.float32)).astype(x.dtype)


def causal_depthwise_conv(x, w):
    k, c = w.shape
    return lax.conv_general_dilated(
        x, w[:, None, :].astype(x.dtype), window_strides=(1,), padding=[(k - 1, 0)],
        dimension_numbers=("NWC", "WIO", "NWC"), feature_group_count=c)


def short_conv_mixer(xv, bg, cg, w_conv):
    return bg * causal_depthwise_conv(cg * xv, w_conv)


def conformer_conv_mixer(a, gate, w_dw, b_dw, ln_g, ln_b):
    u = a * jax.nn.sigmoid(gate)
    u = causal_depthwise_conv(u, w_dw) + b_dw
    return jax.nn.silu(layer_norm(u, ln_g, ln_b))


def peer_route(q, sub_keys):
    s1 = jnp.einsum("thd,hnd->thn", q[..., :PEER_D_HALF], sub_keys[:, 0])
    s2 = jnp.einsum("thd,hnd->thn", q[..., PEER_D_HALF:], sub_keys[:, 1])
    v1, i1 = lax.top_k(s1, PEER_TOPK)
    v2, i2 = lax.top_k(s2, PEER_TOPK)
    cand = (v1[..., :, None] + v2[..., None, :]).reshape(q.shape[:-1] + (PEER_TOPK * PEER_TOPK,))
    vals, pos = lax.top_k(cand, PEER_TOPK)
    e1 = jnp.take_along_axis(i1, pos // PEER_TOPK, axis=-1)
    e2 = jnp.take_along_axis(i2, pos % PEER_TOPK, axis=-1)
    idx = e1 * PEER_N_KEYS + e2
    gates = jax.nn.softmax(vals.astype(jnp.float32), axis=-1).astype(q.dtype)
    return idx, gates


def peer_experts(xc, idx, gates, u, v):
    u_sel = jnp.take(u, idx, axis=0)
    act = jax.nn.gelu(jnp.einsum("chkd,cd->chk", u_sel, xc))
    v_sel = jnp.take(v, idx, axis=0)
    return jnp.einsum("chk,chkd->cd", act * gates, v_sel)


def peer_layer(x, w_q, sub_keys, u, v):
    b, t, d = x.shape
    n_tok = b * t
    pad = (-n_tok) % PEER_CHUNK
    xf = jnp.pad(x.reshape(n_tok, d), ((0, pad), (0, 0)))
    q = (xf @ w_q).reshape(n_tok + pad, PEER_HEADS, PEER_D_KEY)
    idx, gates = peer_route(q, sub_keys)
    n_chunks = (n_tok + pad) // PEER_CHUNK
    xc = xf.reshape(n_chunks, PEER_CHUNK, d)
    idxc = idx.reshape(n_chunks, PEER_CHUNK, PEER_HEADS, PEER_TOPK)
    gc = gates.reshape(n_chunks, PEER_CHUNK, PEER_HEADS, PEER_TOPK)
    out = lax.map(lambda args: peer_experts(args[0], args[1], args[2], u, v), (xc, idxc, gc))
    return out.reshape(n_tok + pad, d)[:n_tok].reshape(b, t, d)


def setup_inputs(seed: int = 0) -> dict:
    key = jax.random.key(seed)
    ks = jax.random.split(key, 20)
    f32 = jnp.float32
    nrm = lambda k, shape, s: (jax.random.normal(k, shape, f32) * s)
    gain = lambda k, shape: 1.0 + 0.02 * jax.random.normal(k, shape, f32)
    return {
        "x": nrm(ks[0], (BATCH, SEQ, D_MODEL), 1.0),
        "meta_tokens": nrm(ks[1], (N_META, D_MODEL), 1.0),
        "norm_mix_g": gain(ks[2], (DEPTH, D_MODEL)),
        "w_in": nrm(ks[3], (DEPTH, D_MODEL, D_IN), D_MODEL ** -0.5),
        "sc_conv_w": nrm(ks[4], (DEPTH, SC_WIDTH, D_SC), SC_WIDTH ** -0.5),
        "cf_conv_w": nrm(ks[5], (DEPTH, CF_WIDTH, D_CF), CF_WIDTH ** -0.5),
        "cf_conv_b": nrm(ks[6], (DEPTH, D_CF), 0.02),
        "cf_ln_g": gain(ks[7], (DEPTH, D_CF)),
        "cf_ln_b": nrm(ks[8], (DEPTH, D_CF), 0.02),
        "out_norm_g_sc": gain(ks[9], (DEPTH, D_SC)),
        "out_norm_g_cf": gain(ks[10], (DEPTH, D_CF)),
        "w_out": nrm(ks[11], (DEPTH, D_MIX, D_MODEL), D_MIX ** -0.5),
        "norm_ffn_g": gain(ks[12], (DEPTH, D_MODEL)),
        "peer_w_q": nrm(ks[13], (DEPTH, D_MODEL, PEER_HEADS * PEER_D_KEY), D_MODEL ** -0.5),
        "peer_sub_keys": nrm(ks[14], (DEPTH, PEER_HEADS, 2, PEER_N_KEYS, PEER_D_HALF), PEER_D_HALF ** -0.5),
        "peer_u": nrm(ks[15], (DEPTH, PEER_N_EXPERTS, D_MODEL), D_MODEL ** -0.5),
        "peer_v": nrm(ks[16], (DEPTH, PEER_N_EXPERTS, D_MODEL), PEER_HEADS ** -0.5),
        "final_norm_g": gain(ks[17], (D_MODEL,)),
    }


def reference(x, meta_tokens, norm_mix_g, w_in, sc_conv_w, cf_conv_w, cf_conv_b, cf_ln_g, cf_ln_b,
              out_norm_g_sc, out_norm_g_cf, w_out, norm_ffn_g, peer_w_q, peer_sub_keys, peer_u, peer_v,
              final_norm_g):
    b = x.shape[0]
    meta = jnp.broadcast_to(meta_tokens[None].astype(x.dtype), (b, N_META, x.shape[-1]))
    h = jnp.concatenate([meta, x], axis=1)
    splits = [D_SC, 2 * D_SC, 3 * D_SC, 3 * D_SC + D_CF]
    for l in range(DEPTH):
        hn = rms_norm(h, norm_mix_g[l])
        proj = hn @ w_in[l]
        xv, bg, cg, ga, gg = jnp.split(proj, splits, axis=-1)
        y_sc = short_conv_mixer(xv, bg, cg, sc_conv_w[l])
        y_cf = conformer_conv_mixer(ga, gg, cf_conv_w[l], cf_conv_b[l], cf_ln_g[l], cf_ln_b[l])
        y = jnp.concatenate([rms_norm(y_sc, out_norm_g_sc[l]), rms_norm(y_cf, out_norm_g_cf[l])], axis=-1)
        h = h + y @ w_out[l]
        hn = rms_norm(h, norm_ffn_g[l])
        h = h + peer_layer(hn, peer_w_q[l], peer_sub_keys[l], peer_u[l], peer_v[l])
    h = rms_norm(h, final_norm_g)
    return h[:, N_META:]
```

```python
import jax
import jax.numpy as jnp
from jax import lax
from jax.experimental import pallas as pl
from jax.experimental.pallas import tpu as pltpu

N_META = 16
D_SC = 512
D_CF = 512
SC_WIDTH = 3
CF_WIDTH = 31
PEER_HEADS = 8
PEER_N_KEYS = 128
PEER_D_HALF = 128
PEER_TOPK = 16
N_SEL = PEER_HEADS * PEER_TOPK
EPS = 1e-6

LANES = 128
SUBLANES = 8
HALO = 32
MIX_TILE = 256
EXP_TILE = 256
GROUP = SUBLANES
BLOCK = 4 * GROUP
BLOCK_IDS = BLOCK * N_SEL
RING = 2
N_TILES = 2 * GROUP
WORDS = 512
ROW_CHUNKS = WORDS // LANES
CHUNK_STRIDE = N_SEL + SUBLANES
VMEM_LIMIT_MIXER = 48 * 1024 * 1024
VMEM_LIMIT_EXPERT = 56 * 1024 * 1024


def _rms(x, g):
    return x * lax.rsqrt(jnp.mean(x * x, axis=-1, keepdims=True) + EPS) * g


def _sigmoid(x):
    return 1.0 / (1.0 + jnp.exp(-x))


def _split3(x):
    a = x.astype(jnp.bfloat16)
    r = x - a.astype(jnp.float32)
    b = r.astype(jnp.bfloat16)
    c = (r - b.astype(jnp.float32)).astype(jnp.bfloat16)
    return a, b, c


def _dot_exact(x, m):
    a, b, c = _split3(x)
    f = lambda t: jnp.dot(t, m, preferred_element_type=jnp.float32)
    return f(a) + f(b) + f(c)


def _top16(s, code, big):
    vals, codes = [], []
    for _ in range(PEER_TOPK):
        m = jnp.max(s, axis=0, keepdims=True)
        c = jnp.min(jnp.where(s == m, code, big), axis=0, keepdims=True)
        vals.append(m)
        codes.append(c)
        s = jnp.where(code == c, -jnp.inf, s)
    return jnp.concatenate(vals, axis=0), jnp.concatenate(codes, axis=0)


def _select_rows(table, sel):
    out = jnp.zeros(sel.shape, table.dtype)
    for a in range(PEER_TOPK):
        out = jnp.where(sel == a, table[a:a + 1, :], out)
    return out


def _route_head(q1, q2, k1, k2):
    t = q1.shape[0]
    nt = (((1,), (1,)), ((), ()))
    s1 = lax.dot_general(k1, q1, nt, preferred_element_type=jnp.float32)
    s2 = lax.dot_general(k2, q2, nt, preferred_element_type=jnp.float32)
    key_id = lax.broadcasted_iota(jnp.int32, (PEER_N_KEYS, t), 0)
    v1, i1 = _top16(s1, key_id, PEER_N_KEYS)
    v2, i2 = _top16(s2, key_id, PEER_N_KEYS)
    sub = lax.broadcasted_iota(jnp.int32, (SUBLANES, t), 0)
    blocks = [v1[0:1, :] + v2[0:8, :], v1[0:1, :] + v2[8:16, :]]
    codes = [sub, sub + 8]
    for a in range(1, 8):
        nb = PEER_TOPK // (a + 1)
        blocks.append(jnp.where(sub < nb, v1[a:a + 1, :] + v2[0:8, :], -jnp.inf))
        codes.append(sub + a * PEER_TOPK)
    blocks.append(v1[8:16, :] + v2[0:1, :])
    codes.append((sub + 8) * PEER_TOPK)
    cand = jnp.concatenate(blocks, axis=0)
    code = jnp.concatenate(codes, axis=0)
    vals, pos = _top16(cand, code, PEER_TOPK * PEER_TOPK)
    e1 = _select_rows(i1, lax.shift_right_logical(pos, 4))
    e2 = _select_rows(i2, lax.bitwise_and(pos, PEER_TOPK - 1))
    ex = jnp.exp(vals - vals[0:1, :])
    gates = ex / jnp.sum(ex, axis=0, keepdims=True)
    return (e1 * PEER_N_KEYS + e2) * ROW_CHUNKS, gates


def _mixer_kernel(x_ref, meta_ref, g_mix_ref, w_in_ref, scw_ref, cfw_ref, cfb_ref, lng_ref, lnb_ref,
                  gsc_ref, gcf_ref, w_out_ref, g_ffn_ref, wq_ref, keys_ref,
                  h1_ref, idx_ref, gate_ref, cx_buf, u_buf):
    ts = x_ref.shape[1]
    s = pl.program_id(1)

    def conv_inputs(rows):
        hn = _rms(rows, g_mix_ref[...]).astype(jnp.bfloat16)
        proj = jnp.dot(hn, w_in_ref[...], preferred_element_type=jnp.float32)
        xv = proj[:, 0:D_SC]
        bg = proj[:, D_SC:2 * D_SC]
        cg = proj[:, 2 * D_SC:3 * D_SC]
        ga = proj[:, 3 * D_SC:3 * D_SC + D_CF]
        gg = proj[:, 3 * D_SC + D_CF:]
        return cg * xv, ga * _sigmoid(gg), bg

    @pl.when(s == 0)
    def _():
        cx_m, u_m, _ = conv_inputs(meta_ref[...])
        zeros = jnp.zeros((HALO - N_META, D_SC), jnp.float32)
        cx_buf[0:HALO - N_META, :] = zeros
        u_buf[0:HALO - N_META, :] = zeros
        cx_buf[HALO - N_META:HALO, :] = cx_m
        u_buf[HALO - N_META:HALO, :] = u_m

    @pl.when(s > 0)
    def _():
        cx_buf[0:HALO, :] = cx_buf[ts:ts + HALO, :]
        u_buf[0:HALO, :] = u_buf[ts:ts + HALO, :]

    x = x_ref[0]
    cx, u, bg = conv_inputs(x)
    cx_buf[HALO:HALO + ts, :] = cx
    u_buf[HALO:HALO + ts, :] = u

    def causal_conv(buf, w_ref, width):
        acc = None
        for r in range(SUBLANES):
            taps = [j for j in range(width) if (HALO - (width - 1) + j) % SUBLANES == r]
            if not taps:
                continue
            length = ts + HALO - (SUBLANES if r else 0)
            shifted = buf[r:r + length, :]
            for j in taps:
                start = HALO - (width - 1) + j - r
                term = w_ref[j:j + 1, :] * shifted[start:start + ts, :]
                acc = term if acc is None else acc + term
        return acc

    y_sc = bg * causal_conv(cx_buf, scw_ref, SC_WIDTH)
    c = causal_conv(u_buf, cfw_ref, CF_WIDTH) + cfb_ref[...]
    mu = jnp.mean(c, axis=-1, keepdims=True)
    var = jnp.mean(jnp.square(c - mu), axis=-1, keepdims=True)
    ln = (c - mu) * lax.rsqrt(var + EPS) * lng_ref[...] + lnb_ref[...]
    y_cf = ln * _sigmoid(ln)
    y = jnp.concatenate([_rms(y_sc, gsc_ref[...]), _rms(y_cf, gcf_ref[...])], axis=-1)
    h1 = x + jnp.dot(y.astype(jnp.bfloat16), w_out_ref[...], preferred_element_type=jnp.float32)
    h1_ref[0] = h1

    hn2 = _rms(h1, g_ffn_ref[...]).astype(jnp.bfloat16)
    q = jnp.dot(hn2, wq_ref[...], preferred_element_type=jnp.float32).astype(jnp.bfloat16)
    ids, gates = [], []
    for h in range(PEER_HEADS):
        base = h * 2 * PEER_D_HALF
        e, g = _route_head(q[:, base:base + PEER_D_HALF], q[:, base + PEER_D_HALF:base + 2 * PEER_D_HALF],
                           keys_ref[h, 0], keys_ref[h, 1])
        ids.append(e)
        gates.append(g)
    ids_t = jnp.concatenate(ids, axis=0)
    gates_t = jnp.concatenate(gates, axis=0)
    idx_ref[0] = pltpu.bitcast(jnp.transpose(pltpu.bitcast(ids_t, jnp.float32)), jnp.int32)
    gate_ref[0] = jnp.transpose(gates_t)


def _gather_rows(idx_smem, base, tab_ref, tile_ref):
    for k in range(N_SEL):
        first_row = pl.multiple_of(idx_smem[base + k], ROW_CHUNKS)
        tile_ref[pl.ds(k, ROW_CHUNKS, stride=CHUNK_STRIDE), :] = tab_ref[pl.ds(first_row, ROW_CHUNKS), :]


def _tile_rows(tile_ref):
    chunks = [pltpu.bitcast(tile_ref[j * CHUNK_STRIDE:j * CHUNK_STRIDE + N_SEL, :], jnp.bfloat16)
              for j in range(ROW_CHUNKS)]
    return jnp.concatenate(chunks, axis=-1)


def _block_idx_copy(idx_hbm, idx_smem, sem, block, slot):
    return pltpu.make_async_copy(idx_hbm.at[pl.ds(pl.multiple_of(block * BLOCK_IDS, BLOCK_IDS), BLOCK_IDS)],
                                 idx_smem.at[pl.ds(slot * BLOCK_IDS, BLOCK_IDS)], sem.at[slot])


def _for_each_block(idx_hbm, idx_smem, sem, blocks_per_step, body):
    assert blocks_per_step % RING == 0
    i = pl.program_id(0)
    total = pl.num_programs(0) * blocks_per_step
    first = i * blocks_per_step

    @pl.when(i == 0)
    def _():
        for s in range(RING - 1):
            _block_idx_copy(idx_hbm, idx_smem, sem, s, s).start()

    def ring_body(it, carry):
        for s in range(RING):
            blk = it * RING + s
            _block_idx_copy(idx_hbm, idx_smem, sem, first + blk, s).wait()
            ahead = first + blk + RING - 1

            @pl.when(ahead < total)
            def _():
                _block_idx_copy(idx_hbm, idx_smem, sem, ahead, (s + RING - 1) % RING).start()

            body(blk, s * BLOCK_IDS)
        return carry

    lax.fori_loop(0, blocks_per_step // RING, ring_body, 0)


def _lane_parity(shape):
    return lax.bitwise_and(lax.broadcasted_iota(jnp.int32, shape, len(shape) - 1), 1)


def _act_kernel(idx_hbm, h1_ref, g_ffn_ref, gate_ref, tab_ref, c_ref,
                idx_smem, sem, tile_ref, xs_ref, raw_ref):
    tc = h1_ref.shape[0]

    hn2 = _rms(h1_ref[...], g_ffn_ref[...])
    hi = hn2.astype(jnp.bfloat16).astype(jnp.float32)
    lo = hn2 - hi
    for g in range(tc // GROUP):
        rows8 = slice(g * GROUP, (g + 1) * GROUP)
        xs_ref[g] = jnp.concatenate([hi[rows8, :WORDS], hi[rows8, WORDS:], lo[rows8, :WORDS], lo[rows8, WORDS:]],
                                    axis=0).astype(jnp.bfloat16)

    odd = _lane_parity((GROUP, 2 * N_SEL)) == 1
    token = lax.broadcasted_iota(jnp.int32, (GROUP, 2 * N_SEL), 0)
    nt = (((1,), (1,)), ((), ()))

    def block_body(blk, smem_offset):
        for q in range(BLOCK // GROUP):
            g = blk * (BLOCK // GROUP) + q
            raw = jnp.zeros((GROUP, 2 * N_SEL), jnp.float32)
            for r in range(GROUP):
                t = q * GROUP + r
                tile = tile_ref.at[t % N_TILES]
                _gather_rows(idx_smem, smem_offset + t * N_SEL, tab_ref, tile)
                acc = lax.dot_general(xs_ref[g], _tile_rows(tile), nt, preferred_element_type=jnp.float32)
                low = acc[0:GROUP] + acc[2 * GROUP:3 * GROUP]
                high = acc[GROUP:2 * GROUP] + acc[3 * GROUP:4 * GROUP]
                raw = jnp.where(token == r, jnp.where(odd, high, low), raw)
            raw_ref[pl.ds(pl.multiple_of(g * GROUP, GROUP), GROUP), :] = raw

    _for_each_block(idx_hbm, idx_smem, sem, tc // BLOCK, block_body)

    rows = lax.broadcasted_iota(jnp.int32, (2 * N_SEL, N_SEL), 0)
    cols = lax.broadcasted_iota(jnp.int32, (2 * N_SEL, N_SEL), 1)
    pair_sum = (lax.shift_right_logical(rows, 1) == cols).astype(jnp.bfloat16)
    act = _dot_exact(raw_ref[...], pair_sum)
    c_ref[...] = jax.nn.gelu(act) * gate_ref[...]


def _out_kernel(idx_hbm, h1_ref, c_ref, g_fin_ref, tab_ref, out_ref,
                idx_smem, sem, tile_ref, cs_ref, peer_ref):
    tc = h1_ref.shape[0]

    rows = lax.broadcasted_iota(jnp.int32, (N_SEL, 2 * N_SEL), 0)
    cols = lax.broadcasted_iota(jnp.int32, (N_SEL, 2 * N_SEL), 1)
    duplicate = (lax.shift_right_logical(cols, 1) == rows).astype(jnp.bfloat16)
    c_dup = _dot_exact(c_ref[...], duplicate)
    odd = _lane_parity((tc, 2 * N_SEL)) == 1
    hi = c_dup.astype(jnp.bfloat16).astype(jnp.float32)
    lo = c_dup - hi
    zero = jnp.zeros_like(hi)
    parts = [jnp.where(odd, zero, hi), jnp.where(odd, hi, zero), jnp.where(odd, zero, lo), jnp.where(odd, lo, zero)]
    for g in range(tc // GROUP):
        r0 = g * GROUP
        cs_ref[g] = jnp.concatenate([p[r0:r0 + GROUP] for p in parts], axis=0).astype(jnp.bfloat16)

    token = lax.broadcasted_iota(jnp.int32, (GROUP, WORDS), 0)

    def block_body(blk, smem_offset):
        for q in range(BLOCK // GROUP):
            g = blk * (BLOCK // GROUP) + q
            low = jnp.zeros((GROUP, WORDS), jnp.float32)
            high = jnp.zeros((GROUP, WORDS), jnp.float32)
            for r in range(GROUP):
                t = q * GROUP + r
                tile = tile_ref.at[t % N_TILES]
                _gather_rows(idx_smem, smem_offset + t * N_SEL, tab_ref, tile)
                res = jnp.dot(cs_ref[g], _tile_rows(tile), preferred_element_type=jnp.float32)
                low = jnp.where(token == r, res[0:GROUP] + res[2 * GROUP:3 * GROUP], low)
                high = jnp.where(token == r, res[GROUP:2 * GROUP] + res[3 * GROUP:4 * GROUP], high)
            rows8 = pl.ds(pl.multiple_of(g * GROUP, GROUP), GROUP)
            peer_ref[rows8, 0:WORDS] = low
            peer_ref[rows8, WORDS:2 * WORDS] = high

    _for_each_block(idx_hbm, idx_smem, sem, tc // BLOCK, block_body)
    out_ref[...] = _rms(h1_ref[...] + peer_ref[...], g_fin_ref[...])


def _pack_table(w):
    e = w.shape[0]
    wb = w.astype(jnp.bfloat16)
    pairs = jnp.stack([wb[:, :WORDS], wb[:, WORDS:]], axis=-1)
    return lax.bitcast_convert_type(pairs, jnp.int32).reshape(e * ROW_CHUNKS, LANES)


def _full(shape):
    return pl.BlockSpec(shape, lambda *_: (0,) * len(shape))


def _mixer_call(x, meta, g_mix, w_in, scw, cfw, cfb, lng, lnb, gsc, gcf, w_out, g_ffn, wq, keys):
    b, s, d = x.shape
    ts = MIX_TILE
    n_tiles = b * (s // ts)
    row = lambda v: v.reshape(1, -1)
    args = (x, meta, row(g_mix), w_in.astype(jnp.bfloat16), scw, cfw, row(cfb), row(lng), row(lnb), row(gsc),
            row(gcf), w_out.astype(jnp.bfloat16), row(g_ffn), wq.astype(jnp.bfloat16), keys.astype(jnp.bfloat16))
    in_specs = [pl.BlockSpec((1, ts, d), lambda i, j: (i, j, 0))] + [_full(a.shape) for a in args[1:]]
    tile_of = lambda i, j: (i * (s // ts) + j, 0, 0)
    return pl.pallas_call(
        _mixer_kernel,
        grid=(b, s // ts),
        in_specs=in_specs,
        out_specs=[pl.BlockSpec((1, ts, d), tile_of), pl.BlockSpec((1, ts, N_SEL), tile_of),
                   pl.BlockSpec((1, ts, N_SEL), tile_of)],
        out_shape=[jax.ShapeDtypeStruct((n_tiles, ts, d), jnp.float32),
                   jax.ShapeDtypeStruct((n_tiles, ts, N_SEL), jnp.int32),
                   jax.ShapeDtypeStruct((n_tiles, ts, N_SEL), jnp.float32)],
        scratch_shapes=[pltpu.VMEM((HALO + ts, D_SC), jnp.float32), pltpu.VMEM((HALO + ts, D_CF), jnp.float32)],
        compiler_params=pltpu.CompilerParams(dimension_semantics=("arbitrary", "arbitrary"),
                                             vmem_limit_bytes=VMEM_LIMIT_MIXER),
        name="mixer_router",
    )(*args)


def _expert_scratch():
    return [pltpu.SMEM((RING * BLOCK_IDS,), jnp.int32), pltpu.SemaphoreType.DMA((RING,)),
            pltpu.VMEM((N_TILES, ROW_CHUNKS * CHUNK_STRIDE, LANES), jnp.int32)]


def _act_call(idx, h1, g_ffn, gates, table):
    n, d = h1.shape
    tc = EXP_TILE
    tok = lambda w: pl.BlockSpec((tc, w), lambda i: (i, 0))
    return pl.pallas_call(
        _act_kernel,
        grid=(n // tc,),
        in_specs=[pl.BlockSpec(memory_space=pl.ANY), tok(d), _full((1, d)), tok(N_SEL), _full(table.shape)],
        out_specs=tok(N_SEL),
        out_shape=jax.ShapeDtypeStruct((n, N_SEL), jnp.float32),
        scratch_shapes=_expert_scratch() + [
            pltpu.VMEM((tc // GROUP, 4 * GROUP, WORDS), jnp.bfloat16),
            pltpu.VMEM((tc, 2 * N_SEL), jnp.float32)],
        compiler_params=pltpu.CompilerParams(dimension_semantics=("arbitrary",),
                                             vmem_limit_bytes=VMEM_LIMIT_EXPERT),
        name="expert_act",
    )(idx, h1, g_ffn.reshape(1, d), gates, table)


def _out_call(idx, h1, c, g_fin, table):
    n, d = h1.shape
    tc = EXP_TILE
    tok = lambda w: pl.BlockSpec((tc, w), lambda i: (i, 0))
    return pl.pallas_call(
        _out_kernel,
        grid=(n // tc,),
        in_specs=[pl.BlockSpec(memory_space=pl.ANY), tok(d), tok(N_SEL), _full((1, d)), _full(table.shape)],
        out_specs=tok(d),
        out_shape=jax.ShapeDtypeStruct((n, d), jnp.float32),
        scratch_shapes=_expert_scratch() + [
            pltpu.VMEM((tc // GROUP, 4 * GROUP, 2 * N_SEL), jnp.bfloat16),
            pltpu.VMEM((tc, d), jnp.float32)],
        compiler_params=pltpu.CompilerParams(dimension_semantics=("arbitrary",),
                                             vmem_limit_bytes=VMEM_LIMIT_EXPERT),
        name="expert_out",
    )(idx, h1, c, g_fin.reshape(1, d), table)


def kernel(x, meta_tokens, norm_mix_g, w_in, sc_conv_w, cf_conv_w, cf_conv_b, cf_ln_g, cf_ln_b, out_norm_g_sc,
           out_norm_g_cf, w_out, norm_ffn_g, peer_w_q, peer_sub_keys, peer_u, peer_v, final_norm_g):
    b, s, d = x.shape
    assert s % MIX_TILE == 0 and (b * s) % EXP_TILE == 0 and EXP_TILE % (RING * BLOCK) == 0
    assert norm_mix_g.shape[0] == 1, "one layer"
    h1, idx, gates = _mixer_call(x, meta_tokens, norm_mix_g[0], w_in[0], sc_conv_w[0], cf_conv_w[0], cf_conv_b[0],
                                 cf_ln_g[0], cf_ln_b[0], out_norm_g_sc[0], out_norm_g_cf[0], w_out[0],
                                 norm_ffn_g[0], peer_w_q[0], peer_sub_keys[0])
    n = b * s
    h1 = h1.reshape(n, d)
    idx = idx.reshape(n * N_SEL)
    gates = gates.reshape(n, N_SEL)
    c = _act_call(idx, h1, norm_ffn_g[0], gates, _pack_table(peer_u[0]))
    out = _out_call(idx, h1, c, final_norm_g, _pack_table(peer_v[0]))
    return out.reshape(b, s, d)
```

```python
import jax
import jax.numpy as jnp
from jax import lax
from jax.experimental import pallas as pl
from jax.experimental.pallas import tpu as pltpu

N_META = 16
D_SC = 512
D_CF = 512
SC_WIDTH = 3
CF_WIDTH = 31
PEER_HEADS = 8
PEER_N_KEYS = 128
PEER_D_HALF = 128
PEER_TOPK = 16
N_SEL = PEER_HEADS * PEER_TOPK
EPS = 1e-6

LANES = 128
SUBLANES = 8
HALO = 32
MIX_TILE = 256
EXP_TILE = 256
GROUP = SUBLANES
BLOCK = 4 * GROUP
BLOCK_IDS = BLOCK * N_SEL
RING = 2
N_TILES = 2 * GROUP
WORDS = 512
ROW_CHUNKS = WORDS // LANES
CHUNK_STRIDE = N_SEL + 1
TILE_ROWS = -(-ROW_CHUNKS * CHUNK_STRIDE // SUBLANES) * SUBLANES
VMEM_LIMIT_MIXER = 48 * 1024 * 1024
VMEM_LIMIT_EXPERT = 56 * 1024 * 1024


def _rms(x, g):
    return x * lax.rsqrt(jnp.mean(x * x, axis=-1, keepdims=True) + EPS) * g


def _sigmoid(x):
    return 1.0 / (1.0 + jnp.exp(-x))


def _split3(x):
    a = x.astype(jnp.bfloat16)
    r = x - a.astype(jnp.float32)
    b = r.astype(jnp.bfloat16)
    c = (r - b.astype(jnp.float32)).astype(jnp.bfloat16)
    return a, b, c


def _dot_exact(x, m):
    a, b, c = _split3(x)
    f = lambda t: jnp.dot(t, m, preferred_element_type=jnp.float32)
    return f(a) + f(b) + f(c)


def _top16(s, code, big):
    vals, codes = [], []
    for _ in range(PEER_TOPK):
        m = jnp.max(s, axis=0, keepdims=True)
        c = jnp.min(jnp.where(s == m, code, big), axis=0, keepdims=True)
        vals.append(m)
        codes.append(c)
        s = jnp.where(code == c, -jnp.inf, s)
    return jnp.concatenate(vals, axis=0), jnp.concatenate(codes, axis=0)


def _select_rows(table, sel):
    out = jnp.zeros(sel.shape, table.dtype)
    for a in range(PEER_TOPK):
        out = jnp.where(sel == a, table[a:a + 1, :], out)
    return out


def _route_head(q1, q2, k1, k2):
    t = q1.shape[0]
    nt = (((1,), (1,)), ((), ()))
    s1 = lax.dot_general(k1, q1, nt, preferred_element_type=jnp.float32)
    s2 = lax.dot_general(k2, q2, nt, preferred_element_type=jnp.float32)
    key_id = lax.broadcasted_iota(jnp.int32, (PEER_N_KEYS, t), 0).astype(jnp.float32)
    v1, i1 = _top16(s1, key_id, float(PEER_N_KEYS))
    v2, i2 = _top16(s2, key_id, float(PEER_N_KEYS))
    sub = lax.broadcasted_iota(jnp.int32, (SUBLANES, t), 0)
    blocks = [v1[0:1, :] + v2[0:8, :], v1[0:1, :] + v2[8:16, :]]
    subf = sub.astype(jnp.float32)
    codes = [subf, subf + 8.0]
    for a in range(1, 8):
        nb = PEER_TOPK // (a + 1)
        blocks.append(jnp.where(sub < nb, v1[a:a + 1, :] + v2[0:8, :], -jnp.inf))
        codes.append(subf + float(a * PEER_TOPK))
    blocks.append(v1[8:16, :] + v2[0:1, :])
    codes.append((subf + 8.0) * float(PEER_TOPK))
    cand = jnp.concatenate(blocks, axis=0)
    code = jnp.concatenate(codes, axis=0)
    vals, pos = _top16(cand, code, float(PEER_TOPK * PEER_TOPK))
    pos = pos.astype(jnp.int32)
    e1 = _select_rows(i1, lax.shift_right_logical(pos, 4))
    e2 = _select_rows(i2, lax.bitwise_and(pos, PEER_TOPK - 1))
    ex = jnp.exp(vals - vals[0:1, :])
    gates = ex / jnp.sum(ex, axis=0, keepdims=True)
    return ((e1 * float(PEER_N_KEYS) + e2) * float(ROW_CHUNKS)).astype(jnp.int32), gates


def _mixer_kernel(x_ref, meta_ref, g_mix_ref, w_in_ref, scw_ref, cfw_ref, cfb_ref, lng_ref, lnb_ref,
                  gsc_ref, gcf_ref, w_out_ref, g_ffn_ref, wq_ref, keys_ref,
                  h1_ref, idx_ref, gate_ref, cx_buf, u_buf):
    ts = x_ref.shape[1]
    s = pl.program_id(1)

    def conv_inputs(rows):
        hn = _rms(rows, g_mix_ref[...]).astype(jnp.bfloat16)
        proj = jnp.dot(hn, w_in_ref[...], preferred_element_type=jnp.float32)
        xv = proj[:, 0:D_SC]
        bg = proj[:, D_SC:2 * D_SC]
        cg = proj[:, 2 * D_SC:3 * D_SC]
        ga = proj[:, 3 * D_SC:3 * D_SC + D_CF]
        gg = proj[:, 3 * D_SC + D_CF:]
        return cg * xv, ga * _sigmoid(gg), bg

    @pl.when(s == 0)
    def _():
        cx_m, u_m, _ = conv_inputs(meta_ref[...])
        zeros = jnp.zeros((HALO - N_META, D_SC), jnp.float32)
        cx_buf[0:HALO - N_META, :] = zeros
        u_buf[0:HALO - N_META, :] = zeros
        cx_buf[HALO - N_META:HALO, :] = cx_m
        u_buf[HALO - N_META:HALO, :] = u_m

    @pl.when(s > 0)
    def _():
        cx_buf[0:HALO, :] = cx_buf[ts:ts + HALO, :]
        u_buf[0:HALO, :] = u_buf[ts:ts + HALO, :]

    x = x_ref[0]
    cx, u, bg = conv_inputs(x)
    cx_buf[HALO:HALO + ts, :] = cx
    u_buf[HALO:HALO + ts, :] = u

    def causal_conv(buf, w_ref, width):
        acc = None
        for r in range(SUBLANES):
            taps = [j for j in range(width) if (HALO - (width - 1) + j) % SUBLANES == r]
            if not taps:
                continue
            length = ts + HALO - (SUBLANES if r else 0)
            shifted = buf[r:r + length, :]
            for j in taps:
                start = HALO - (width - 1) + j - r
                term = w_ref[j:j + 1, :] * shifted[start:start + ts, :]
                acc = term if acc is None else acc + term
        return acc

    y_sc = bg * causal_conv(cx_buf, scw_ref, SC_WIDTH)
    c = causal_conv(u_buf, cfw_ref, CF_WIDTH) + cfb_ref[...]
    mu = jnp.mean(c, axis=-1, keepdims=True)
    var = jnp.mean(jnp.square(c - mu), axis=-1, keepdims=True)
    ln = (c - mu) * lax.rsqrt(var + EPS) * lng_ref[...] + lnb_ref[...]
    y_cf = ln * _sigmoid(ln)
    y = jnp.concatenate([_rms(y_sc, gsc_ref[...]), _rms(y_cf, gcf_ref[...])], axis=-1)
    h1 = x + jnp.dot(y.astype(jnp.bfloat16), w_out_ref[...], preferred_element_type=jnp.float32)
    h1_ref[0] = h1

    hn2 = _rms(h1, g_ffn_ref[...]).astype(jnp.bfloat16)
    q = jnp.dot(hn2, wq_ref[...], preferred_element_type=jnp.float32).astype(jnp.bfloat16)
    ids, gates = [], []
    for h in range(PEER_HEADS):
        base = h * 2 * PEER_D_HALF
        e, g = _route_head(q[:, base:base + PEER_D_HALF], q[:, base + PEER_D_HALF:base + 2 * PEER_D_HALF],
                           keys_ref[h, 0], keys_ref[h, 1])
        ids.append(e)
        gates.append(g)
    ids_t = jnp.concatenate(ids, axis=0)
    gates_t = jnp.concatenate(gates, axis=0)
    idx_ref[0] = pltpu.bitcast(jnp.transpose(pltpu.bitcast(ids_t, jnp.float32)), jnp.int32)
    gate_ref[0] = jnp.transpose(gates_t)


def _gather_rows(idx_smem, base, tab_ref, tile_ref):
    for k in range(N_SEL):
        first_row = pl.multiple_of(idx_smem[base + k], ROW_CHUNKS)
        tile_ref[pl.ds(k, ROW_CHUNKS, stride=CHUNK_STRIDE), :] = tab_ref[pl.ds(first_row, ROW_CHUNKS), :]


def _tile_rows(tile_ref):
    chunks = [pltpu.bitcast(tile_ref[pl.ds(j * CHUNK_STRIDE, N_SEL, stride=1), :], jnp.bfloat16)
              for j in range(ROW_CHUNKS)]
    return jnp.concatenate(chunks, axis=-1)


def _block_idx_copy(idx_hbm, idx_smem, sem, block, slot):
    return pltpu.make_async_copy(idx_hbm.at[pl.ds(pl.multiple_of(block * BLOCK_IDS, BLOCK_IDS), BLOCK_IDS)],
                                 idx_smem.at[pl.ds(slot * BLOCK_IDS, BLOCK_IDS)], sem.at[slot])


def _for_each_block(idx_hbm, idx_smem, sem, blocks_per_step, body):
    assert blocks_per_step % RING == 0
    i = pl.program_id(0)
    total = pl.num_programs(0) * blocks_per_step
    first = i * blocks_per_step

    @pl.when(i == 0)
    def _():
        for s in range(RING - 1):
            _block_idx_copy(idx_hbm, idx_smem, sem, s, s).start()

    def ring_body(it, carry):
        for s in range(RING):
            blk = it * RING + s
            _block_idx_copy(idx_hbm, idx_smem, sem, first + blk, s).wait()
            ahead = first + blk + RING - 1

            @pl.when(ahead < total)
            def _():
                _block_idx_copy(idx_hbm, idx_smem, sem, ahead, (s + RING - 1) % RING).start()

            body(blk, s * BLOCK_IDS)
        return carry

    lax.fori_loop(0, blocks_per_step // RING, ring_body, 0)


def _lane_parity(shape):
    return lax.bitwise_and(lax.broadcasted_iota(jnp.int32, shape, len(shape) - 1), 1)


def _act_kernel(idx_hbm, h1_ref, g_ffn_ref, gate_ref, tab_ref, c_ref,
                idx_smem, sem, tile_ref, xs_ref, raw_ref):
    tc = h1_ref.shape[0]

    hn2 = _rms(h1_ref[...], g_ffn_ref[...])
    hi = hn2.astype(jnp.bfloat16).astype(jnp.float32)
    lo = hn2 - hi
    for g in range(tc // GROUP):
        rows8 = slice(g * GROUP, (g + 1) * GROUP)
        xs_ref[g] = jnp.concatenate([hi[rows8, :WORDS], hi[rows8, WORDS:], lo[rows8, :WORDS], lo[rows8, WORDS:]],
                                    axis=0).astype(jnp.bfloat16)

    odd = _lane_parity((GROUP, 2 * N_SEL)) == 1
    token = lax.broadcasted_iota(jnp.int32, (GROUP, 2 * N_SEL), 0)
    nt = (((1,), (1,)), ((), ()))

    def block_body(blk, smem_offset):
        for q in range(BLOCK // GROUP):
            g = blk * (BLOCK // GROUP) + q
            raw = jnp.zeros((GROUP, 2 * N_SEL), jnp.float32)
            for r in range(GROUP):
                t = q * GROUP + r
                tile = tile_ref.at[t % N_TILES]
                _gather_rows(idx_smem, smem_offset + t * N_SEL, tab_ref, tile)
                acc = lax.dot_general(xs_ref[g], _tile_rows(tile), nt, preferred_element_type=jnp.float32)
                low = acc[0:GROUP] + acc[2 * GROUP:3 * GROUP]
                high = acc[GROUP:2 * GROUP] + acc[3 * GROUP:4 * GROUP]
                raw = jnp.where(token == r, jnp.where(odd, high, low), raw)
            raw_ref[pl.ds(pl.multiple_of(g * GROUP, GROUP), GROUP), :] = raw

    _for_each_block(idx_hbm, idx_smem, sem, tc // BLOCK, block_body)

    rows = lax.broadcasted_iota(jnp.int32, (2 * N_SEL, N_SEL), 0)
    cols = lax.broadcasted_iota(jnp.int32, (2 * N_SEL, N_SEL), 1)
    pair_sum = (lax.shift_right_logical(rows, 1) == cols).astype(jnp.bfloat16)
    act = _dot_exact(raw_ref[...], pair_sum)
    c_ref[...] = jax.nn.gelu(act) * gate_ref[...]


def _out_kernel(idx_hbm, h1_ref, c_ref, g_fin_ref, tab_ref, out_ref,
                idx_smem, sem, tile_ref, cs_ref, peer_ref):
    tc = h1_ref.shape[0]

    rows = lax.broadcasted_iota(jnp.int32, (N_SEL, 2 * N_SEL), 0)
    cols = lax.broadcasted_iota(jnp.int32, (N_SEL, 2 * N_SEL), 1)
    duplicate = (lax.shift_right_logical(cols, 1) == rows).astype(jnp.bfloat16)
    c_dup = _dot_exact(c_ref[...], duplicate)
    odd = _lane_parity((tc, 2 * N_SEL)) == 1
    hi = c_dup.astype(jnp.bfloat16).astype(jnp.float32)
    lo = c_dup - hi
    zero = jnp.zeros_like(hi)
    parts = [jnp.where(odd, zero, hi), jnp.where(odd, hi, zero), jnp.where(odd, zero, lo), jnp.where(odd, lo, zero)]
    for g in range(tc // GROUP):
        r0 = g * GROUP
        cs_ref[g] = jnp.concatenate([p[r0:r0 + GROUP] for p in parts], axis=0).astype(jnp.bfloat16)

    token = lax.broadcasted_iota(jnp.int32, (GROUP, WORDS), 0)

    def block_body(blk, smem_offset):
        for q in range(BLOCK // GROUP):
            g = blk * (BLOCK // GROUP) + q
            low = jnp.zeros((GROUP, WORDS), jnp.float32)
            high = jnp.zeros((GROUP, WORDS), jnp.float32)
            for r in range(GROUP):
                t = q * GROUP + r
                tile = tile_ref.at[t % N_TILES]
                _gather_rows(idx_smem, smem_offset + t * N_SEL, tab_ref, tile)
                res = jnp.dot(cs_ref[g], _tile_rows(tile), preferred_element_type=jnp.float32)
                low = jnp.where(token == r, res[0:GROUP] + res[2 * GROUP:3 * GROUP], low)
                high = jnp.where(token == r, res[GROUP:2 * GROUP] + res[3 * GROUP:4 * GROUP], high)
            rows8 = pl.ds(pl.multiple_of(g * GROUP, GROUP), GROUP)
            peer_ref[rows8, 0:WORDS] = low
            peer_ref[rows8, WORDS:2 * WORDS] = high

    _for_each_block(idx_hbm, idx_smem, sem, tc // BLOCK, block_body)
    out_ref[...] = _rms(h1_ref[...] + peer_ref[...], g_fin_ref[...])


def _pack_table(w):
    e = w.shape[0]
    wb = w.astype(jnp.bfloat16)
    pairs = jnp.stack([wb[:, :WORDS], wb[:, WORDS:]], axis=-1)
    return lax.bitcast_convert_type(pairs, jnp.int32).reshape(e * ROW_CHUNKS, LANES)


def _full(shape):
    return pl.BlockSpec(shape, lambda *_: (0,) * len(shape))


def _mixer_call(x, meta, g_mix, w_in, scw, cfw, cfb, lng, lnb, gsc, gcf, w_out, g_ffn, wq, keys):
    b, s, d = x.shape
    ts = MIX_TILE
    n_tiles = b * (s // ts)
    row = lambda v: v.reshape(1, -1)
    args = (x, meta, row(g_mix), w_in.astype(jnp.bfloat16), scw, cfw, row(cfb), row(lng), row(lnb), row(gsc),
            row(gcf), w_out.astype(jnp.bfloat16), row(g_ffn), wq.astype(jnp.bfloat16), keys.astype(jnp.bfloat16))
    in_specs = [pl.BlockSpec((1, ts, d), lambda i, j: (i, j, 0))] + [_full(a.shape) for a in args[1:]]
    tile_of = lambda i, j: (i * (s // ts) + j, 0, 0)
    return pl.pallas_call(
        _mixer_kernel,
        grid=(b, s // ts),
        in_specs=in_specs,
        out_specs=[pl.BlockSpec((1, ts, d), tile_of), pl.BlockSpec((1, ts, N_SEL), tile_of),
                   pl.BlockSpec((1, ts, N_SEL), tile_of)],
        out_shape=[jax.ShapeDtypeStruct((n_tiles, ts, d), jnp.float32),
                   jax.ShapeDtypeStruct((n_tiles, ts, N_SEL), jnp.int32),
                   jax.ShapeDtypeStruct((n_tiles, ts, N_SEL), jnp.float32)],
        scratch_shapes=[pltpu.VMEM((HALO + ts, D_SC), jnp.float32), pltpu.VMEM((HALO + ts, D_CF), jnp.float32)],
        compiler_params=pltpu.CompilerParams(dimension_semantics=("arbitrary", "arbitrary"),
                                             vmem_limit_bytes=VMEM_LIMIT_MIXER),
        name="mixer_router",
    )(*args)


def _expert_scratch():
    return [pltpu.SMEM((RING * BLOCK_IDS,), jnp.int32), pltpu.SemaphoreType.DMA((RING,)),
            pltpu.VMEM((N_TILES, TILE_ROWS, LANES), jnp.int32)]


def _act_call(idx, h1, g_ffn, gates, table):
    n, d = h1.shape
    tc = EXP_TILE
    tok = lambda w: pl.BlockSpec((tc, w), lambda i: (i, 0))
    return pl.pallas_call(
        _act_kernel,
        grid=(n // tc,),
        in_specs=[pl.BlockSpec(memory_space=pl.ANY), tok(d), _full((1, d)), tok(N_SEL), _full(table.shape)],
        out_specs=tok(N_SEL),
        out_shape=jax.ShapeDtypeStruct((n, N_SEL), jnp.float32),
        scratch_shapes=_expert_scratch() + [
            pltpu.VMEM((tc // GROUP, 4 * GROUP, WORDS), jnp.bfloat16),
            pltpu.VMEM((tc, 2 * N_SEL), jnp.float32)],
        compiler_params=pltpu.CompilerParams(dimension_semantics=("arbitrary",),
                                             vmem_limit_bytes=VMEM_LIMIT_EXPERT),
        name="expert_act",
    )(idx, h1, g_ffn.reshape(1, d), gates, table)


def _out_call(idx, h1, c, g_fin, table):
    n, d = h1.shape
    tc = EXP_TILE
    tok = lambda w: pl.BlockSpec((tc, w), lambda i: (i, 0))
    return pl.pallas_call(
        _out_kernel,
        grid=(n // tc,),
        in_specs=[pl.BlockSpec(memory_space=pl.ANY), tok(d), tok(N_SEL), _full((1, d)), _full(table.shape)],
        out_specs=tok(d),
        out_shape=jax.ShapeDtypeStruct((n, d), jnp.float32),
        scratch_shapes=_expert_scratch() + [
            pltpu.VMEM((tc // GROUP, 4 * GROUP, 2 * N_SEL), jnp.bfloat16),
            pltpu.VMEM((tc, d), jnp.float32)],
        compiler_params=pltpu.CompilerParams(dimension_semantics=("arbitrary",),
                                             vmem_limit_bytes=VMEM_LIMIT_EXPERT),
        name="expert_out",
    )(idx, h1, c, g_fin.reshape(1, d), table)


def kernel(x, meta_tokens, norm_mix_g, w_in, sc_conv_w, cf_conv_w, cf_conv_b, cf_ln_g, cf_ln_b, out_norm_g_sc,
           out_norm_g_cf, w_out, norm_ffn_g, peer_w_q, peer_sub_keys, peer_u, peer_v, final_norm_g):
    b, s, d = x.shape
    assert s % MIX_TILE == 0 and (b * s) % EXP_TILE == 0 and EXP_TILE % (RING * BLOCK) == 0
    assert norm_mix_g.shape[0] == 1, "one layer"
    h1, idx, gates = _mixer_call(x, meta_tokens, norm_mix_g[0], w_in[0], sc_conv_w[0], cf_conv_w[0], cf_conv_b[0],
                                 cf_ln_g[0], cf_ln_b[0], out_norm_g_sc[0], out_norm_g_cf[0], w_out[0],
                                 norm_ffn_g[0], peer_w_q[0], peer_sub_keys[0])
    n = b * s
    h1 = h1.reshape(n, d)
    idx = idx.reshape(n * N_SEL)
    gates = gates.reshape(n, N_SEL)
    c = _act_call(idx, h1, norm_ffn_g[0], gates, _pack_table(peer_u[0]))
    out = _out_call(idx, h1, c, final_norm_g, _pack_table(peer_v[0]))
    return out.reshape(b, s, d)
```

```python
import jax
import jax.numpy as jnp
from jax import lax
from jax.experimental import pallas as pl
from jax.experimental.pallas import tpu as pltpu

N_META = 16
D_SC = 512
D_CF = 512
SC_WIDTH = 3
CF_WIDTH = 31
PEER_HEADS = 8
PEER_N_KEYS = 128
PEER_D_HALF = 128
PEER_TOPK = 16
N_SEL = PEER_HEADS * PEER_TOPK
EPS = 1e-6

LANES = 128
SUBLANES = 8
HALO = 32
MIX_TILE = 256
EXP_TILE = 256
GROUP = SUBLANES
BLOCK = 8 * GROUP
BLOCK_IDS = BLOCK * N_SEL
RING = 2
N_TILES = 2 * GROUP
WORDS = 512
ROW_CHUNKS = WORDS // LANES
HALF_SEL = N_SEL // 2
PLANE_STRIDE = HALF_SEL + ROW_CHUNKS
TILE_ROWS = 2 * ROW_CHUNKS * PLANE_STRIDE
TABLE_PAD = ROW_CHUNKS
VMEM_LIMIT_MIXER = 48 * 1024 * 1024
VMEM_LIMIT_EXPERT = 56 * 1024 * 1024


def _rms(x, g):
    return x * lax.rsqrt(jnp.mean(x * x, axis=-1, keepdims=True) + EPS) * g


def _sigmoid(x):
    return 1.0 / (1.0 + jnp.exp(-x))


def _split3(x):
    a = x.astype(jnp.bfloat16)
    r = x - a.astype(jnp.float32)
    b = r.astype(jnp.bfloat16)
    c = (r - b.astype(jnp.float32)).astype(jnp.bfloat16)
    return a, b, c


def _dot_exact(x, m):
    a, b, c = _split3(x)
    f = lambda t: jnp.dot(t, m, preferred_element_type=jnp.float32)
    return f(a) + f(b) + f(c)


def _top16(s, code, big):
    vals, codes = [], []
    for _ in range(PEER_TOPK):
        m = jnp.max(s, axis=0, keepdims=True)
        c = jnp.min(jnp.where(s == m, code, big), axis=0, keepdims=True)
        vals.append(m)
        codes.append(c)
        s = jnp.where(code == c, -jnp.inf, s)
    return jnp.concatenate(vals, axis=0), jnp.concatenate(codes, axis=0)


def _select_rows(table, sel):
    out = jnp.zeros(sel.shape, table.dtype)
    for a in range(PEER_TOPK):
        out = jnp.where(sel == a, table[a:a + 1, :], out)
    return out


def _route_head(q1, q2, k1, k2):
    t = q1.shape[0]
    nt = (((1,), (1,)), ((), ()))
    s1 = lax.dot_general(k1, q1, nt, preferred_element_type=jnp.float32)
    s2 = lax.dot_general(k2, q2, nt, preferred_element_type=jnp.float32)
    key_id = lax.broadcasted_iota(jnp.int32, (PEER_N_KEYS, t), 0).astype(jnp.float32)
    v1, i1 = _top16(s1, key_id, float(PEER_N_KEYS))
    v2, i2 = _top16(s2, key_id, float(PEER_N_KEYS))
    sub = lax.broadcasted_iota(jnp.int32, (SUBLANES, t), 0)
    blocks = [v1[0:1, :] + v2[0:8, :], v1[0:1, :] + v2[8:16, :]]
    subf = sub.astype(jnp.float32)
    codes = [subf, subf + 8.0]
    for a in range(1, 8):
        nb = PEER_TOPK // (a + 1)
        blocks.append(jnp.where(sub < nb, v1[a:a + 1, :] + v2[0:8, :], -jnp.inf))
        codes.append(subf + float(a * PEER_TOPK))
    blocks.append(v1[8:16, :] + v2[0:1, :])
    codes.append((subf + 8.0) * float(PEER_TOPK))
    cand = jnp.concatenate(blocks, axis=0)
    code = jnp.concatenate(codes, axis=0)
    vals, pos = _top16(cand, code, float(PEER_TOPK * PEER_TOPK))
    pos = pos.astype(jnp.int32)
    e1 = _select_rows(i1, lax.shift_right_logical(pos, 4))
    e2 = _select_rows(i2, lax.bitwise_and(pos, PEER_TOPK - 1))
    ex = jnp.exp(vals - vals[0:1, :])
    gates = ex / jnp.sum(ex, axis=0, keepdims=True)
    return ((e1 * float(PEER_N_KEYS) + e2) * float(ROW_CHUNKS) + float(TABLE_PAD)).astype(jnp.int32), gates


def _mixer_kernel(x_ref, meta_ref, g_mix_ref, w_in_ref, scw_ref, cfw_ref, cfb_ref, lng_ref, lnb_ref,
                  gsc_ref, gcf_ref, w_out_ref, g_ffn_ref, wq_ref, keys_ref,
                  h1_ref, idx_ref, gate_ref, cx_buf, u_buf):
    ts = x_ref.shape[1]
    s = pl.program_id(1)

    def conv_inputs(rows):
        hn = _rms(rows, g_mix_ref[...]).astype(jnp.bfloat16)
        proj = jnp.dot(hn, w_in_ref[...], preferred_element_type=jnp.float32)
        xv = proj[:, 0:D_SC]
        bg = proj[:, D_SC:2 * D_SC]
        cg = proj[:, 2 * D_SC:3 * D_SC]
        ga = proj[:, 3 * D_SC:3 * D_SC + D_CF]
        gg = proj[:, 3 * D_SC + D_CF:]
        return cg * xv, ga * _sigmoid(gg), bg

    @pl.when(s == 0)
    def _():
        cx_m, u_m, _ = conv_inputs(meta_ref[...])
        zeros = jnp.zeros((HALO - N_META, D_SC), jnp.float32)
        cx_buf[0:HALO - N_META, :] = zeros
        u_buf[0:HALO - N_META, :] = zeros
        cx_buf[HALO - N_META:HALO, :] = cx_m
        u_buf[HALO - N_META:HALO, :] = u_m

    @pl.when(s > 0)
    def _():
        cx_buf[0:HALO, :] = cx_buf[ts:ts + HALO, :]
        u_buf[0:HALO, :] = u_buf[ts:ts + HALO, :]

    x = x_ref[0]
    cx, u, bg = conv_inputs(x)
    cx_buf[HALO:HALO + ts, :] = cx
    u_buf[HALO:HALO + ts, :] = u

    def causal_conv(buf, w_ref, width):
        acc = None
        for r in range(SUBLANES):
            taps = [j for j in range(width) if (HALO - (width - 1) + j) % SUBLANES == r]
            if not taps:
                continue
            length = ts + HALO - (SUBLANES if r else 0)
            shifted = buf[r:r + length, :]
            for j in taps:
                start = HALO - (width - 1) + j - r
                term = w_ref[j:j + 1, :] * shifted[start:start + ts, :]
                acc = term if acc is None else acc + term
        return acc

    y_sc = bg * causal_conv(cx_buf, scw_ref, SC_WIDTH)
    c = causal_conv(u_buf, cfw_ref, CF_WIDTH) + cfb_ref[...]
    mu = jnp.mean(c, axis=-1, keepdims=True)
    var = jnp.mean(jnp.square(c - mu), axis=-1, keepdims=True)
    ln = (c - mu) * lax.rsqrt(var + EPS) * lng_ref[...] + lnb_ref[...]
    y_cf = ln * _sigmoid(ln)
    y = jnp.concatenate([_rms(y_sc, gsc_ref[...]), _rms(y_cf, gcf_ref[...])], axis=-1)
    h1 = x + jnp.dot(y.astype(jnp.bfloat16), w_out_ref[...], preferred_element_type=jnp.float32)
    h1_ref[0] = h1

    hn2 = _rms(h1, g_ffn_ref[...]).astype(jnp.bfloat16)
    q = jnp.dot(hn2, wq_ref[...], preferred_element_type=jnp.float32).astype(jnp.bfloat16)
    ids, gates = [], []
    for h in range(PEER_HEADS):
        base = h * 2 * PEER_D_HALF
        e, g = _route_head(q[:, base:base + PEER_D_HALF], q[:, base + PEER_D_HALF:base + 2 * PEER_D_HALF],
                           keys_ref[h, 0], keys_ref[h, 1])
        ids.append(e)
        gates.append(g)
    ids_t = jnp.concatenate(ids, axis=0)
    gates_t = jnp.concatenate(gates, axis=0)
    idx_ref[0] = pltpu.bitcast(jnp.transpose(pltpu.bitcast(ids_t, jnp.float32)), jnp.int32)
    gate_ref[0] = jnp.transpose(gates_t)


def _gather_rows(idx_smem, base, tab_ref, tile_ref):
    lower = lax.broadcasted_iota(jnp.int32, (2 * SUBLANES, LANES), 0) < 2 * ROW_CHUNKS
    for m in range(HALF_SEL):
        row_a = idx_smem[base + 2 * m]
        row_b = idx_smem[base + 2 * m + 1] - ROW_CHUNKS
        both = jnp.where(lower, pltpu.bitcast(tab_ref[pl.ds(row_a, SUBLANES, stride=1), :], jnp.bfloat16),
                         pltpu.bitcast(tab_ref[pl.ds(row_b, SUBLANES, stride=1), :], jnp.bfloat16))
        tile_ref[pl.ds(m, SUBLANES, stride=PLANE_STRIDE), :] = pltpu.bitcast(both, jnp.int32)


def _tile_rows(tile_ref):
    def plane(p):
        return tile_ref[pl.ds(p * PLANE_STRIDE, HALF_SEL, stride=1), :]
    chunks = [pltpu.bitcast(jnp.concatenate([plane(j), plane(ROW_CHUNKS + j)], axis=0), jnp.bfloat16)
              for j in range(ROW_CHUNKS)]
    return jnp.concatenate(chunks, axis=-1)


def _tile_order(p):
    return jnp.where(p < HALF_SEL, 2 * p, 2 * p - (N_SEL - 1))


def _block_idx_copy(idx_hbm, idx_smem, sem, block, slot):
    return pltpu.make_async_copy(idx_hbm.at[pl.ds(pl.multiple_of(block * BLOCK_IDS, BLOCK_IDS), BLOCK_IDS)],
                                 idx_smem.at[pl.ds(slot * BLOCK_IDS, BLOCK_IDS)], sem.at[slot])


def _for_each_block(idx_hbm, idx_smem, sem, blocks_per_step, body):
    assert blocks_per_step % RING == 0
    i = pl.program_id(0)
    total = pl.num_programs(0) * blocks_per_step
    first = i * blocks_per_step

    @pl.when(i == 0)
    def _():
        for s in range(RING - 1):
            _block_idx_copy(idx_hbm, idx_smem, sem, s, s).start()

    def ring_body(it, carry):
        for s in range(RING):
            blk = it * RING + s
            _block_idx_copy(idx_hbm, idx_smem, sem, first + blk, s).wait()
            ahead = first + blk + RING - 1

            @pl.when(ahead < total)
            def _():
                _block_idx_copy(idx_hbm, idx_smem, sem, ahead, (s + RING - 1) % RING).start()

            body(blk, s * BLOCK_IDS)
        return carry

    lax.fori_loop(0, blocks_per_step // RING, ring_body, 0)


def _lane_parity(shape):
    return lax.bitwise_and(lax.broadcasted_iota(jnp.int32, shape, len(shape) - 1), 1)


def _act_kernel(idx_hbm, h1_ref, g_ffn_ref, gate_ref, tab_ref, c_ref,
                idx_smem, sem, tile_ref, xs_ref, raw_ref):
    tc = h1_ref.shape[0]

    hn2 = _rms(h1_ref[...], g_ffn_ref[...])
    hi = hn2.astype(jnp.bfloat16).astype(jnp.float32)
    lo = hn2 - hi
    for g in range(tc // GROUP):
        rows8 = slice(g * GROUP, (g + 1) * GROUP)
        xs_ref[g] = jnp.concatenate([hi[rows8, :WORDS], hi[rows8, WORDS:], lo[rows8, :WORDS], lo[rows8, WORDS:]],
                                    axis=0).astype(jnp.bfloat16)

    odd = _lane_parity((GROUP, 2 * N_SEL)) == 1
    token = lax.broadcasted_iota(jnp.int32, (GROUP, 2 * N_SEL), 0)
    nt = (((1,), (1,)), ((), ()))

    def block_body(blk, smem_offset):
        for q in range(BLOCK // GROUP):
            g = blk * (BLOCK // GROUP) + q
            raw = jnp.zeros((GROUP, 2 * N_SEL), jnp.float32)
            for r in range(GROUP):
                t = q * GROUP + r
                tile = tile_ref.at[t % N_TILES]
                _gather_rows(idx_smem, smem_offset + t * N_SEL, tab_ref, tile)
                rows = _tile_rows(tile)
                acc = jnp.concatenate(
                    [lax.dot_general(xs_ref[g], rows[:N_SEL], nt, preferred_element_type=jnp.float32),
                     lax.dot_general(xs_ref[g], rows[N_SEL:], nt, preferred_element_type=jnp.float32)], axis=-1)
                low = acc[0:GROUP] + acc[2 * GROUP:3 * GROUP]
                high = acc[GROUP:2 * GROUP] + acc[3 * GROUP:4 * GROUP]
                raw = jnp.where(token == r, jnp.where(odd, high, low), raw)
            raw_ref[pl.ds(pl.multiple_of(g * GROUP, GROUP), GROUP), :] = raw

    _for_each_block(idx_hbm, idx_smem, sem, tc // BLOCK, block_body)

    rows = lax.broadcasted_iota(jnp.int32, (2 * N_SEL, N_SEL), 0)
    cols = lax.broadcasted_iota(jnp.int32, (2 * N_SEL, N_SEL), 1)
    pair_sum = (_tile_order(lax.shift_right_logical(rows, 1)) == cols).astype(jnp.bfloat16)
    act = _dot_exact(raw_ref[...], pair_sum)
    c_ref[...] = jax.nn.gelu(act) * gate_ref[...]


def _out_kernel(idx_hbm, h1_ref, c_ref, g_fin_ref, tab_ref, out_ref,
                idx_smem, sem, tile_ref, cs_ref, peer_ref):
    tc = h1_ref.shape[0]

    rows = lax.broadcasted_iota(jnp.int32, (N_SEL, 2 * N_SEL), 0)
    cols = lax.broadcasted_iota(jnp.int32, (N_SEL, 2 * N_SEL), 1)
    duplicate = (_tile_order(lax.shift_right_logical(cols, 1)) == rows).astype(jnp.bfloat16)
    c_dup = _dot_exact(c_ref[...], duplicate)
    odd = _lane_parity((tc, 2 * N_SEL)) == 1
    hi = c_dup.astype(jnp.bfloat16).astype(jnp.float32)
    lo = c_dup - hi
    zero = jnp.zeros_like(hi)
    parts = [jnp.where(odd, zero, hi), jnp.where(odd, hi, zero), jnp.where(odd, zero, lo), jnp.where(odd, lo, zero)]
    for g in range(tc // GROUP):
        r0 = g * GROUP
        cs_ref[g] = jnp.concatenate([p[r0:r0 + GROUP] for p in parts], axis=0).astype(jnp.bfloat16)

    token = lax.broadcasted_iota(jnp.int32, (GROUP, WORDS), 0)

    def block_body(blk, smem_offset):
        for q in range(BLOCK // GROUP):
            g = blk * (BLOCK // GROUP) + q
            low = jnp.zeros((GROUP, WORDS), jnp.float32)
            high = jnp.zeros((GROUP, WORDS), jnp.float32)
            for r in range(GROUP):
                t = q * GROUP + r
                tile = tile_ref.at[t % N_TILES]
                _gather_rows(idx_smem, smem_offset + t * N_SEL, tab_ref, tile)
                res = jnp.dot(cs_ref[g], _tile_rows(tile), preferred_element_type=jnp.float32)
                low = jnp.where(token == r, res[0:GROUP] + res[2 * GROUP:3 * GROUP], low)
                high = jnp.where(token == r, res[GROUP:2 * GROUP] + res[3 * GROUP:4 * GROUP], high)
            rows8 = pl.ds(pl.multiple_of(g * GROUP, GROUP), GROUP)
            peer_ref[rows8, 0:WORDS] = low
            peer_ref[rows8, WORDS:2 * WORDS] = high

    _for_each_block(idx_hbm, idx_smem, sem, tc // BLOCK, block_body)
    out_ref[...] = _rms(h1_ref[...] + peer_ref[...], g_fin_ref[...])


def _pack_table(w):
    e = w.shape[0]
    wb = w.astype(jnp.bfloat16)
    pairs = jnp.stack([wb[:, :WORDS], wb[:, WORDS:]], axis=-1)
    words = lax.bitcast_convert_type(pairs, jnp.int32).reshape(e * ROW_CHUNKS, LANES)
    return jnp.pad(words, ((TABLE_PAD, TABLE_PAD), (0, 0)))


def _full(shape):
    return pl.BlockSpec(shape, lambda *_: (0,) * len(shape))


def _mixer_call(x, meta, g_mix, w_in, scw, cfw, cfb, lng, lnb, gsc, gcf, w_out, g_ffn, wq, keys):
    b, s, d = x.shape
    ts = MIX_TILE
    n_tiles = b * (s // ts)
    row = lambda v: v.reshape(1, -1)
    args = (x, meta, row(g_mix), w_in.astype(jnp.bfloat16), scw, cfw, row(cfb), row(lng), row(lnb), row(gsc),
            row(gcf), w_out.astype(jnp.bfloat16), row(g_ffn), wq.astype(jnp.bfloat16), keys.astype(jnp.bfloat16))
    in_specs = [pl.BlockSpec((1, ts, d), lambda i, j: (i, j, 0))] + [_full(a.shape) for a in args[1:]]
    tile_of = lambda i, j: (i * (s // ts) + j, 0, 0)
    return pl.pallas_call(
        _mixer_kernel,
        grid=(b, s // ts),
        in_specs=in_specs,
        out_specs=[pl.BlockSpec((1, ts, d), tile_of), pl.BlockSpec((1, ts, N_SEL), tile_of),
                   pl.BlockSpec((1, ts, N_SEL), tile_of)],
        out_shape=[jax.ShapeDtypeStruct((n_tiles, ts, d), jnp.float32),
                   jax.ShapeDtypeStruct((n_tiles, ts, N_SEL), jnp.int32),
                   jax.ShapeDtypeStruct((n_tiles, ts, N_SEL), jnp.float32)],
        scratch_shapes=[pltpu.VMEM((HALO + ts, D_SC), jnp.float32), pltpu.VMEM((HALO + ts, D_CF), jnp.float32)],
        compiler_params=pltpu.CompilerParams(dimension_semantics=("arbitrary", "arbitrary"),
                                             vmem_limit_bytes=VMEM_LIMIT_MIXER),
        name="mixer_router",
    )(*args)


def _expert_scratch():
    return [pltpu.SMEM((RING * BLOCK_IDS,), jnp.int32), pltpu.SemaphoreType.DMA((RING,)),
            pltpu.VMEM((N_TILES, TILE_ROWS, LANES), jnp.int32)]


def _act_call(idx, h1, g_ffn, gates, table):
    n, d = h1.shape
    tc = EXP_TILE
    tok = lambda w: pl.BlockSpec((tc, w), lambda i: (i, 0))
    return pl.pallas_call(
        _act_kernel,
        grid=(n // tc,),
        in_specs=[pl.BlockSpec(memory_space=pl.ANY), tok(d), _full((1, d)), tok(N_SEL), _full(table.shape)],
        out_specs=tok(N_SEL),
        out_shape=jax.ShapeDtypeStruct((n, N_SEL), jnp.float32),
        scratch_shapes=_expert_scratch() + [
            pltpu.VMEM((tc // GROUP, 4 * GROUP, WORDS), jnp.bfloat16),
            pltpu.VMEM((tc, 2 * N_SEL), jnp.float32)],
        compiler_params=pltpu.CompilerParams(dimension_semantics=("arbitrary",),
                                             vmem_limit_bytes=VMEM_LIMIT_EXPERT),
        name="expert_act",
    )(idx, h1, g_ffn.reshape(1, d), gates, table)


def _out_call(idx, h1, c, g_fin, table):
    n, d = h1.shape
    tc = EXP_TILE
    tok = lambda w: pl.BlockSpec((tc, w), lambda i: (i, 0))
    return pl.pallas_call(
        _out_kernel,
        grid=(n // tc,),
        in_specs=[pl.BlockSpec(memory_space=pl.ANY), tok(d), tok(N_SEL), _full((1, d)), _full(table.shape)],
        out_specs=tok(d),
        out_shape=jax.ShapeDtypeStruct((n, d), jnp.float32),
        scratch_shapes=_expert_scratch() + [
            pltpu.VMEM((tc // GROUP, 4 * GROUP, 2 * N_SEL), jnp.bfloat16),
            pltpu.VMEM((tc, d), jnp.float32)],
        compiler_params=pltpu.CompilerParams(dimension_semantics=("arbitrary",),
                                             vmem_limit_bytes=VMEM_LIMIT_EXPERT),
        name="expert_out",
    )(idx, h1, c, g_fin.reshape(1, d), table)


def kernel(x, meta_tokens, norm_mix_g, w_in, sc_conv_w, cf_conv_w, cf_conv_b, cf_ln_g, cf_ln_b, out_norm_g_sc,
           out_norm_g_cf, w_out, norm_ffn_g, peer_w_q, peer_sub_keys, peer_u, peer_v, final_norm_g):
    b, s, d = x.shape
    assert s % MIX_TILE == 0 and (b * s) % EXP_TILE == 0 and EXP_TILE % (RING * BLOCK) == 0
    assert norm_mix_g.shape[0] == 1, "one layer"
    h1, idx, gates = _mixer_call(x, meta_tokens, norm_mix_g[0], w_in[0], sc_conv_w[0], cf_conv_w[0], cf_conv_b[0],
                                 cf_ln_g[0], cf_ln_b[0], out_norm_g_sc[0], out_norm_g_cf[0], w_out[0],
                                 norm_ffn_g[0], peer_w_q[0], peer_sub_keys[0])
    n = b * s
    h1 = h1.reshape(n, d)
    idx = idx.reshape(n * N_SEL)
    gates = gates.reshape(n, N_SEL)
    c = _act_call(idx, h1, norm_ffn_g[0], gates, _pack_table(peer_u[0]))
    out = _out_call(idx, h1, c, final_norm_g, _pack_table(peer_v[0]))
    return out.reshape(b, s, d)
```

```python
import jax
import jax.numpy as jnp
from jax import lax
from jax.experimental import pallas as pl
from jax.experimental.pallas import tpu as pltpu

N_META = 16
D_SC = 512
D_CF = 512
SC_WIDTH = 3
CF_WIDTH = 31
PEER_HEADS = 8
PEER_N_KEYS = 128
PEER_D_HALF = 128
PEER_TOPK = 16
N_SEL = PEER_HEADS * PEER_TOPK
EPS = 1e-6

LANES = 128
SUBLANES = 8
HALO = 32
MIX_TILE = 256
EXP_TILE = 256
GROUP = SUBLANES
BLOCK = 8 * GROUP
BLOCK_IDS = BLOCK * N_SEL
RING = 2
N_TILES = 2 * GROUP
WORDS = 512
ROW_CHUNKS = WORDS // LANES
HALF_SEL = N_SEL // 2
PLANE_STRIDE = HALF_SEL + 1
TILE_ROWS = 2 * ROW_CHUNKS * PLANE_STRIDE
TABLE_PAD = ROW_CHUNKS
VMEM_LIMIT_MIXER = 48 * 1024 * 1024
VMEM_LIMIT_EXPERT = 56 * 1024 * 1024


def _rms(x, g):
    return x * lax.rsqrt(jnp.mean(x * x, axis=-1, keepdims=True) + EPS) * g


def _sigmoid(x):
    return 1.0 / (1.0 + jnp.exp(-x))


def _split3(x):
    a = x.astype(jnp.bfloat16)
    r = x - a.astype(jnp.float32)
    b = r.astype(jnp.bfloat16)
    c = (r - b.astype(jnp.float32)).astype(jnp.bfloat16)
    return a, b, c


def _dot_exact(x, m):
    a, b, c = _split3(x)
    f = lambda t: jnp.dot(t, m, preferred_element_type=jnp.float32)
    return f(a) + f(b) + f(c)


def _top16(s, code, big):
    vals, codes = [], []
    for _ in range(PEER_TOPK):
        m = jnp.max(s, axis=0, keepdims=True)
        c = jnp.min(jnp.where(s == m, code, big), axis=0, keepdims=True)
        vals.append(m)
        codes.append(c)
        s = jnp.where(code == c, -jnp.inf, s)
    return jnp.concatenate(vals, axis=0), jnp.concatenate(codes, axis=0)


def _select_rows(table, sel):
    out = jnp.zeros(sel.shape, table.dtype)
    for a in range(PEER_TOPK):
        out = jnp.where(sel == a, table[a:a + 1, :], out)
    return out


def _route_head(q1, q2, k1, k2):
    t = q1.shape[0]
    nt = (((1,), (1,)), ((), ()))
    s1 = lax.dot_general(k1, q1, nt, preferred_element_type=jnp.float32)
    s2 = lax.dot_general(k2, q2, nt, preferred_element_type=jnp.float32)
    key_id = lax.broadcasted_iota(jnp.int32, (PEER_N_KEYS, t), 0).astype(jnp.float32)
    v1, i1 = _top16(s1, key_id, float(PEER_N_KEYS))
    v2, i2 = _top16(s2, key_id, float(PEER_N_KEYS))
    sub = lax.broadcasted_iota(jnp.int32, (SUBLANES, t), 0)
    blocks = [v1[0:1, :] + v2[0:8, :], v1[0:1, :] + v2[8:16, :]]
    subf = sub.astype(jnp.float32)
    codes = [subf, subf + 8.0]
    for a in range(1, 8):
        nb = PEER_TOPK // (a + 1)
        blocks.append(jnp.where(sub < nb, v1[a:a + 1, :] + v2[0:8, :], -jnp.inf))
        codes.append(subf + float(a * PEER_TOPK))
    blocks.append(v1[8:16, :] + v2[0:1, :])
    codes.append((subf + 8.0) * float(PEER_TOPK))
    cand = jnp.concatenate(blocks, axis=0)
    code = jnp.concatenate(codes, axis=0)
    vals, pos = _top16(cand, code, float(PEER_TOPK * PEER_TOPK))
    pos = pos.astype(jnp.int32)
    e1 = _select_rows(i1, lax.shift_right_logical(pos, 4))
    e2 = _select_rows(i2, lax.bitwise_and(pos, PEER_TOPK - 1))
    ex = jnp.exp(vals - vals[0:1, :])
    gates = ex / jnp.sum(ex, axis=0, keepdims=True)
    return ((e1 * float(PEER_N_KEYS) + e2) * float(ROW_CHUNKS) + float(TABLE_PAD)).astype(jnp.int32), gates


def _mixer_kernel(x_ref, meta_ref, g_mix_ref, w_in_ref, scw_ref, cfw_ref, cfb_ref, lng_ref, lnb_ref,
                  gsc_ref, gcf_ref, w_out_ref, g_ffn_ref, wq_ref, keys_ref,
                  h1_ref, idx_ref, gate_ref, cx_buf, u_buf):
    ts = x_ref.shape[1]
    s = pl.program_id(1)

    def conv_inputs(rows):
        hn = _rms(rows, g_mix_ref[...]).astype(jnp.bfloat16)
        proj = jnp.dot(hn, w_in_ref[...], preferred_element_type=jnp.float32)
        xv = proj[:, 0:D_SC]
        bg = proj[:, D_SC:2 * D_SC]
        cg = proj[:, 2 * D_SC:3 * D_SC]
        ga = proj[:, 3 * D_SC:3 * D_SC + D_CF]
        gg = proj[:, 3 * D_SC + D_CF:]
        return cg * xv, ga * _sigmoid(gg), bg

    @pl.when(s == 0)
    def _():
        cx_m, u_m, _ = conv_inputs(meta_ref[...])
        zeros = jnp.zeros((HALO - N_META, D_SC), jnp.float32)
        cx_buf[0:HALO - N_META, :] = zeros
        u_buf[0:HALO - N_META, :] = zeros
        cx_buf[HALO - N_META:HALO, :] = cx_m
        u_buf[HALO - N_META:HALO, :] = u_m

    @pl.when(s > 0)
    def _():
        cx_buf[0:HALO, :] = cx_buf[ts:ts + HALO, :]
        u_buf[0:HALO, :] = u_buf[ts:ts + HALO, :]

    x = x_ref[0]
    cx, u, bg = conv_inputs(x)
    cx_buf[HALO:HALO + ts, :] = cx
    u_buf[HALO:HALO + ts, :] = u

    def causal_conv(buf, w_ref, width):
        acc = None
        for r in range(SUBLANES):
            taps = [j for j in range(width) if (HALO - (width - 1) + j) % SUBLANES == r]
            if not taps:
                continue
            length = ts + HALO - (SUBLANES if r else 0)
            shifted = buf[r:r + length, :]
            for j in taps:
                start = HALO - (width - 1) + j - r
                term = w_ref[j:j + 1, :] * shifted[start:start + ts, :]
                acc = term if acc is None else acc + term
        return acc

    y_sc = bg * causal_conv(cx_buf, scw_ref, SC_WIDTH)
    c = causal_conv(u_buf, cfw_ref, CF_WIDTH) + cfb_ref[...]
    mu = jnp.mean(c, axis=-1, keepdims=True)
    var = jnp.mean(jnp.square(c - mu), axis=-1, keepdims=True)
    ln = (c - mu) * lax.rsqrt(var + EPS) * lng_ref[...] + lnb_ref[...]
    y_cf = ln * _sigmoid(ln)
    y = jnp.concatenate([_rms(y_sc, gsc_ref[...]), _rms(y_cf, gcf_ref[...])], axis=-1)
    h1 = x + jnp.dot(y.astype(jnp.bfloat16), w_out_ref[...], preferred_element_type=jnp.float32)
    h1_ref[0] = h1

    hn2 = _rms(h1, g_ffn_ref[...]).astype(jnp.bfloat16)
    q = jnp.dot(hn2, wq_ref[...], preferred_element_type=jnp.float32).astype(jnp.bfloat16)
    ids, gates = [], []
    for h in range(PEER_HEADS):
        base = h * 2 * PEER_D_HALF
        e, g = _route_head(q[:, base:base + PEER_D_HALF], q[:, base + PEER_D_HALF:base + 2 * PEER_D_HALF],
                           keys_ref[h, 0], keys_ref[h, 1])
        ids.append(e)
        gates.append(g)
    ids_t = jnp.concatenate(ids, axis=0)
    gates_t = jnp.concatenate(gates, axis=0)
    idx_ref[0] = pltpu.bitcast(jnp.transpose(pltpu.bitcast(ids_t, jnp.float32)), jnp.int32)
    gate_ref[0] = jnp.transpose(gates_t)


def _gather_rows(idx_smem, base, tab_ref, tile_ref):
    lower = lax.broadcasted_iota(jnp.int32, (2 * SUBLANES, LANES), 0) < 2 * ROW_CHUNKS
    for m in range(HALF_SEL):
        row_a = idx_smem[base + 2 * m]
        row_b = idx_smem[base + 2 * m + 1] - ROW_CHUNKS
        both = jnp.where(lower, pltpu.bitcast(tab_ref[pl.ds(row_a, SUBLANES, stride=1), :], jnp.bfloat16),
                         pltpu.bitcast(tab_ref[pl.ds(row_b, SUBLANES, stride=1), :], jnp.bfloat16))
        tile_ref[pl.ds(m, SUBLANES, stride=PLANE_STRIDE), :] = pltpu.bitcast(both, jnp.int32)


def _tile_rows(tile_ref):
    def plane(p):
        return tile_ref[pl.ds(p * PLANE_STRIDE, HALF_SEL, stride=1), :]
    chunks = [pltpu.bitcast(jnp.concatenate([plane(j), plane(ROW_CHUNKS + j)], axis=0), jnp.bfloat16)
              for j in range(ROW_CHUNKS)]
    return jnp.concatenate(chunks, axis=-1)


def _tile_order(p):
    return jnp.where(p < HALF_SEL, 2 * p, 2 * p - (N_SEL - 1))


def _block_idx_copy(idx_hbm, idx_smem, sem, block, slot):
    return pltpu.make_async_copy(idx_hbm.at[pl.ds(pl.multiple_of(block * BLOCK_IDS, BLOCK_IDS), BLOCK_IDS)],
                                 idx_smem.at[pl.ds(slot * BLOCK_IDS, BLOCK_IDS)], sem.at[slot])


def _for_each_block(idx_hbm, idx_smem, sem, blocks_per_step, body):
    assert blocks_per_step % RING == 0
    i = pl.program_id(0)
    total = pl.num_programs(0) * blocks_per_step
    first = i * blocks_per_step

    @pl.when(i == 0)
    def _():
        for s in range(RING - 1):
            _block_idx_copy(idx_hbm, idx_smem, sem, s, s).start()

    def ring_body(it, carry):
        for s in range(RING):
            blk = it * RING + s
            _block_idx_copy(idx_hbm, idx_smem, sem, first + blk, s).wait()
            ahead = first + blk + RING - 1

            @pl.when(ahead < total)
            def _():
                _block_idx_copy(idx_hbm, idx_smem, sem, ahead, (s + RING - 1) % RING).start()

            body(blk, s * BLOCK_IDS)
        return carry

    lax.fori_loop(0, blocks_per_step // RING, ring_body, 0)


def _lane_parity(shape):
    return lax.bitwise_and(lax.broadcasted_iota(jnp.int32, shape, len(shape) - 1), 1)


def _act_kernel(idx_hbm, h1_ref, g_ffn_ref, gate_ref, tab_ref, c_ref,
                idx_smem, sem, tile_ref, xs_ref, raw_ref):
    tc = h1_ref.shape[0]

    hn2 = _rms(h1_ref[...], g_ffn_ref[...])
    hi = hn2.astype(jnp.bfloat16).astype(jnp.float32)
    lo = hn2 - hi
    for g in range(tc // GROUP):
        rows8 = slice(g * GROUP, (g + 1) * GROUP)
        xs_ref[g] = jnp.concatenate([hi[rows8, :WORDS], hi[rows8, WORDS:], lo[rows8, :WORDS], lo[rows8, WORDS:]],
                                    axis=0).astype(jnp.bfloat16)

    odd = _lane_parity((GROUP, 2 * N_SEL)) == 1
    token = lax.broadcasted_iota(jnp.int32, (GROUP, 2 * N_SEL), 0)
    nt = (((1,), (1,)), ((), ()))

    def block_body(blk, smem_offset):
        for q in range(BLOCK // GROUP):
            g = blk * (BLOCK // GROUP) + q
            raw = jnp.zeros((GROUP, 2 * N_SEL), jnp.float32)
            for r in range(GROUP):
                t = q * GROUP + r
                tile = tile_ref.at[t % N_TILES]
                _gather_rows(idx_smem, smem_offset + t * N_SEL, tab_ref, tile)
                acc = lax.dot_general(xs_ref[g], _tile_rows(tile), nt, preferred_element_type=jnp.float32)
                low = acc[0:GROUP] + acc[2 * GROUP:3 * GROUP]
                high = acc[GROUP:2 * GROUP] + acc[3 * GROUP:4 * GROUP]
                raw = jnp.where(token == r, jnp.where(odd, high, low), raw)
            raw_ref[pl.ds(pl.multiple_of(g * GROUP, GROUP), GROUP), :] = raw

    _for_each_block(idx_hbm, idx_smem, sem, tc // BLOCK, block_body)

    rows = lax.broadcasted_iota(jnp.int32, (2 * N_SEL, N_SEL), 0)
    cols = lax.broadcasted_iota(jnp.int32, (2 * N_SEL, N_SEL), 1)
    pair_sum = (_tile_order(lax.shift_right_logical(rows, 1)) == cols).astype(jnp.bfloat16)
    act = _dot_exact(raw_ref[...], pair_sum)
    c_ref[...] = jax.nn.gelu(act) * gate_ref[...]


def _out_kernel(idx_hbm, h1_ref, c_ref, g_fin_ref, tab_ref, out_ref,
                idx_smem, sem, tile_ref, cs_ref, peer_ref):
    tc = h1_ref.shape[0]

    rows = lax.broadcasted_iota(jnp.int32, (N_SEL, 2 * N_SEL), 0)
    cols = lax.broadcasted_iota(jnp.int32, (N_SEL, 2 * N_SEL), 1)
    duplicate = (_tile_order(lax.shift_right_logical(cols, 1)) == rows).astype(jnp.bfloat16)
    c_dup = _dot_exact(c_ref[...], duplicate)
    odd = _lane_parity((tc, 2 * N_SEL)) == 1
    hi = c_dup.astype(jnp.bfloat16).astype(jnp.float32)
    lo = c_dup - hi
    zero = jnp.zeros_like(hi)
    parts = [jnp.where(odd, zero, hi), jnp.where(odd, hi, zero), jnp.where(odd, zero, lo), jnp.where(odd, lo, zero)]
    for g in range(tc // GROUP):
        r0 = g * GROUP
        cs_ref[g] = jnp.concatenate([p[r0:r0 + GROUP] for p in parts], axis=0).astype(jnp.bfloat16)

    token = lax.broadcasted_iota(jnp.int32, (GROUP, WORDS), 0)

    def block_body(blk, smem_offset):
        for q in range(BLOCK // GROUP):
            g = blk * (BLOCK // GROUP) + q
            low = jnp.zeros((GROUP, WORDS), jnp.float32)
            high = jnp.zeros((GROUP, WORDS), jnp.float32)
            for r in range(GROUP):
                t = q * GROUP + r
                tile = tile_ref.at[t % N_TILES]
                _gather_rows(idx_smem, smem_offset + t * N_SEL, tab_ref, tile)
                res = jnp.dot(cs_ref[g], _tile_rows(tile), preferred_element_type=jnp.float32)
                low = jnp.where(token == r, res[0:GROUP] + res[2 * GROUP:3 * GROUP], low)
                high = jnp.where(token == r, res[GROUP:2 * GROUP] + res[3 * GROUP:4 * GROUP], high)
            rows8 = pl.ds(pl.multiple_of(g * GROUP, GROUP), GROUP)
            peer_ref[rows8, 0:WORDS] = low
            peer_ref[rows8, WORDS:2 * WORDS] = high

    _for_each_block(idx_hbm, idx_smem, sem, tc // BLOCK, block_body)
    out_ref[...] = _rms(h1_ref[...] + peer_ref[...], g_fin_ref[...])


def _pack_table(w):
    e = w.shape[0]
    wb = w.astype(jnp.bfloat16)
    pairs = jnp.stack([wb[:, :WORDS], wb[:, WORDS:]], axis=-1)
    words = lax.bitcast_convert_type(pairs, jnp.int32).reshape(e * ROW_CHUNKS, LANES)
    return jnp.pad(words, ((TABLE_PAD, TABLE_PAD), (0, 0)))


def _full(shape):
    return pl.BlockSpec(shape, lambda *_: (0,) * len(shape))


def _mixer_call(x, meta, g_mix, w_in, scw, cfw, cfb, lng, lnb, gsc, gcf, w_out, g_ffn, wq, keys):
    b, s, d = x.shape
    ts = MIX_TILE
    n_tiles = b * (s // ts)
    row = lambda v: v.reshape(1, -1)
    args = (x, meta, row(g_mix), w_in.astype(jnp.bfloat16), scw, cfw, row(cfb), row(lng), row(lnb), row(gsc),
            row(gcf), w_out.astype(jnp.bfloat16), row(g_ffn), wq.astype(jnp.bfloat16), keys.astype(jnp.bfloat16))
    in_specs = [pl.BlockSpec((1, ts, d), lambda i, j: (i, j, 0))] + [_full(a.shape) for a in args[1:]]
    tile_of = lambda i, j: (i * (s // ts) + j, 0, 0)
    return pl.pallas_call(
        _mixer_kernel,
        grid=(b, s // ts),
        in_specs=in_specs,
        out_specs=[pl.BlockSpec((1, ts, d), tile_of), pl.BlockSpec((1, ts, N_SEL), tile_of),
                   pl.BlockSpec((1, ts, N_SEL), tile_of)],
        out_shape=[jax.ShapeDtypeStruct((n_tiles, ts, d), jnp.float32),
                   jax.ShapeDtypeStruct((n_tiles, ts, N_SEL), jnp.int32),
                   jax.ShapeDtypeStruct((n_tiles, ts, N_SEL), jnp.float32)],
        scratch_shapes=[pltpu.VMEM((HALO + ts, D_SC), jnp.float32), pltpu.VMEM((HALO + ts, D_CF), jnp.float32)],
        compiler_params=pltpu.CompilerParams(dimension_semantics=("arbitrary", "arbitrary"),
                                             vmem_limit_bytes=VMEM_LIMIT_MIXER),
        name="mixer_router",
    )(*args)


def _expert_scratch():
    return [pltpu.SMEM((RING * BLOCK_IDS,), jnp.int32), pltpu.SemaphoreType.DMA((RING,)),
            pltpu.VMEM((N_TILES, TILE_ROWS, LANES), jnp.int32)]


def _act_call(idx, h1, g_ffn, gates, table):
    n, d = h1.shape
    tc = EXP_TILE
    tok = lambda w: pl.BlockSpec((tc, w), lambda i: (i, 0))
    return pl.pallas_call(
        _act_kernel,
        grid=(n // tc,),
        in_specs=[pl.BlockSpec(memory_space=pl.ANY), tok(d), _full((1, d)), tok(N_SEL), _full(table.shape)],
        out_specs=tok(N_SEL),
        out_shape=jax.ShapeDtypeStruct((n, N_SEL), jnp.float32),
        scratch_shapes=_expert_scratch() + [
            pltpu.VMEM((tc // GROUP, 4 * GROUP, WORDS), jnp.bfloat16),
            pltpu.VMEM((tc, 2 * N_SEL), jnp.float32)],
        compiler_params=pltpu.CompilerParams(dimension_semantics=("arbitrary",),
                                             vmem_limit_bytes=VMEM_LIMIT_EXPERT),
        name="expert_act",
    )(idx, h1, g_ffn.reshape(1, d), gates, table)


def _out_call(idx, h1, c, g_fin, table):
    n, d = h1.shape
    tc = EXP_TILE
    tok = lambda w: pl.BlockSpec((tc, w), lambda i: (i, 0))
    return pl.pallas_call(
        _out_kernel,
        grid=(n // tc,),
        in_specs=[pl.BlockSpec(memory_space=pl.ANY), tok(d), tok(N_SEL), _full((1, d)), _full(table.shape)],
        out_specs=tok(d),
        out_shape=jax.ShapeDtypeStruct((n, d), jnp.float32),
        scratch_shapes=_expert_scratch() + [
            pltpu.VMEM((tc // GROUP, 4 * GROUP, 2 * N_SEL), jnp.bfloat16),
            pltpu.VMEM((tc, d), jnp.float32)],
        compiler_params=pltpu.CompilerParams(dimension_semantics=("arbitrary",),
                                             vmem_limit_bytes=VMEM_LIMIT_EXPERT),
        name="expert_out",
    )(idx, h1, c, g_fin.reshape(1, d), table)


def kernel(x, meta_tokens, norm_mix_g, w_in, sc_conv_w, cf_conv_w, cf_conv_b, cf_ln_g, cf_ln_b, out_norm_g_sc,
           out_norm_g_cf, w_out, norm_ffn_g, peer_w_q, peer_sub_keys, peer_u, peer_v, final_norm_g):
    b, s, d = x.shape
    assert s % MIX_TILE == 0 and (b * s) % EXP_TILE == 0 and EXP_TILE % (RING * BLOCK) == 0
    assert norm_mix_g.shape[0] == 1, "one layer"
    h1, idx, gates = _mixer_call(x, meta_tokens, norm_mix_g[0], w_in[0], sc_conv_w[0], cf_conv_w[0], cf_conv_b[0],
                                 cf_ln_g[0], cf_ln_b[0], out_norm_g_sc[0], out_norm_g_cf[0], w_out[0],
                                 norm_ffn_g[0], peer_w_q[0], peer_sub_keys[0])
    n = b * s
    h1 = h1.reshape(n, d)
    idx = idx.reshape(n * N_SEL)
    gates = gates.reshape(n, N_SEL)
    c = _act_call(idx, h1, norm_ffn_g[0], gates, _pack_table(peer_u[0]))
    out = _out_call(idx, h1, c, final_norm_g, _pack_table(peer_v[0]))
    return out.reshape(b, s, d)
```
